```python
import jax, jax.numpy as jnp
from jax import lax
import numpy as np

D_MODEL = 1024
BATCH = 16
SEQ = 256
DEPTH = 1
DEC_BATCH = 4
DEC_SEQ = 4096
PAST_LEN = 512

GRID_W = 64
D_INNER = 2 * D_MODEL
D_SSD = D_INNER // 2
D_CM = D_INNER - D_SSD
SSD_HEAD_DIM = 64
N_SSD_HEADS = D_SSD // SSD_HEAD_DIM
N_SSD_GROUPS = 4
D_STATE = 128
D_CONV = 5
SSD_CHUNK = 128
CM_CHUNK = 128
N_CM_HEADS = 8
CM_HEAD_DIM = D_CM // N_CM_HEADS
D_FF = 4 * D_MODEL
N_MOD = 6
EPS = 1e-6
D_XBC = D_SSD + 2 * N_SSD_GROUPS * D_STATE
D_IN_PROJ = D_SSD + D_XBC + 2 * N_SSD_HEADS + 2 * D_CM
SPLITS = (D_SSD, D_SSD + D_XBC, D_SSD + D_XBC + 2 * N_SSD_HEADS, D_SSD + D_XBC + 2 * N_SSD_HEADS + D_CM)

kernel_name = "hybrid_ssd_chunkmlp_diffusion_step"


def rmsnorm(x, g):
    xf = x.astype(jnp.float32)
    y = xf * lax.rsqrt(jnp.mean(xf * xf, axis=-1, keepdims=True) + EPS)
    return (y * g.astype(jnp.float32)).astype(x.dtype)


def layernorm(x, g, b):
    xf = x.astype(jnp.float32)
    mu = jnp.mean(xf, axis=-1, keepdims=True)
    var = jnp.mean(jnp.square(xf - mu), axis=-1, keepdims=True)
    y = (xf - mu) * lax.rsqrt(var + EPS)
    return (y * g.astype(jnp.float32) + b.astype(jnp.float32)).astype(x.dtype)


def sincos_2d(L, dtype):
    rows = L // GRID_W
    r, col = jnp.meshgrid(jnp.arange(rows, dtype=jnp.float32), jnp.arange(GRID_W, dtype=jnp.float32), indexing='ij')
    r, col = r.reshape(L), col.reshape(L)
    nf = D_MODEL // 4
    omega = 1.0 / (10000.0 ** (jnp.arange(nf, dtype=jnp.float32) / nf))
    ar = r[:, None] * omega
    ac = col[:, None] * omega
    return jnp.concatenate([jnp.sin(ar), jnp.cos(ar), jnp.sin(ac), jnp.cos(ac)], axis=-1).astype(dtype)


def centred_dwconv(x, w, b):
    pad = D_CONV // 2
    L = x.shape[1]
    xp = jnp.pad(x, ((0, 0), (pad, pad), (0, 0)))
    out = b
    for k in range(D_CONV):
        out = out + xp[:, k:k + L] * w[k]
    return out


def ssd_chunked(x, dt, A, Bm, Cm, h0):
    out_dtype = x.dtype
    b, L, H, P = x.shape
    G, N = Bm.shape[2], Bm.shape[3]
    hg = H // G
    Q = SSD_CHUNK
    nc = L // Q
    xc = x.astype(jnp.float32).reshape(b, nc, Q, G, hg, P)
    dtc = dt.reshape(b, nc, Q, G, hg)
    Bc = Bm.astype(jnp.float32).reshape(b, nc, Q, G, N)
    Cc = Cm.astype(jnp.float32).reshape(b, nc, Q, G, N)
    a_cum = jnp.cumsum(dtc * A.reshape(G, hg), axis=2)
    xdt = xc * dtc[..., None]
    seg = a_cum[:, :, :, None] - a_cum[:, :, None, :]
    mask = jnp.tril(jnp.ones((Q, Q), dtype=bool))[None, None, :, :, None, None]
    decay = jnp.exp(jnp.where(mask, seg, -jnp.inf))
    cb = jnp.einsum('bcign,bcjgn->bcijg', Cc, Bc)
    y_diag = jnp.einsum('bcijgh,bcjghp->bcighp', cb[..., None] * decay, xdt)
    decay_to_end = jnp.exp(a_cum[:, :, -1:] - a_cum)
    states = jnp.einsum('bcjgn,bcjghp->bcghpn', Bc, decay_to_end[..., None] * xdt)
    chunk_decay = jnp.exp(a_cum[:, :, -1])

    def step(h, inp):
        s, dA = inp
        return h * dA[..., None, None] + s, h

    h_init = h0.astype(jnp.float32).reshape(b, G, hg, P, N)
    h_final, h_starts = lax.scan(step, h_init, (jnp.moveaxis(states, 1, 0), jnp.moveaxis(chunk_decay, 1, 0)))
    h_starts = jnp.moveaxis(h_starts, 0, 1)
    y_off = jnp.einsum('bcign,bcghpn->bcighp', Cc, h_starts) * jnp.exp(a_cum)[..., None]
    y = (y_diag + y_off).reshape(b, L, H, P)
    return y.astype(out_dtype), h_final.reshape(b, H, P, N).astype(out_dtype)


def ssd_mixer(z, xbc, dt_raw, conv_w, conv_b, dt_bias, A_log, d_skip, norm_g, h0):
    b, L, _ = z.shape
    xbc = jax.nn.silu(centred_dwconv(xbc, conv_w, conv_b))
    xs, Bm, Cm = jnp.split(xbc, (D_SSD, D_SSD + N_SSD_GROUPS * D_STATE), axis=-1)
    xh = xs.reshape(b, L, N_SSD_HEADS, SSD_HEAD_DIM)
    Bm = Bm.reshape(b, L, N_SSD_GROUPS, D_STATE)
    Cm = Cm.reshape(b, L, N_SSD_GROUPS, D_STATE)
    dt = jax.nn.softplus(dt_raw.astype(jnp.float32) + dt_bias.astype(jnp.float32))
    A = -jnp.exp(A_log.astype(jnp.float32))
    flip = lambda t: jnp.flip(t, axis=1)
    y_f, hf_f = ssd_chunked(xh, dt[:, :, 0], A[0], Bm, Cm, h0[:, 0])
    y_b, hf_b = ssd_chunked(flip(xh), flip(dt[:, :, 1]), A[1], flip(Bm), flip(Cm), h0[:, 1])
    y = y_f + flip(y_b) + xh * d_skip[:, None]
    y = y.reshape(b, L, D_SSD)
    out = rmsnorm(y * jax.nn.silu(z), norm_g)
    return out, jnp.stack([hf_f, hf_b], axis=1)


def chunk_mlp(u, v, ln_g, ln_b, w_s, b_s):
    u = jax.nn.gelu(u)
    v = layernorm(jax.nn.gelu(v), ln_g, ln_b)
    b, L, _ = v.shape
    nc = L // CM_CHUNK
    vc = v.reshape(b, nc, CM_CHUNK, N_CM_HEADS, CM_HEAD_DIM)
    mixed = jnp.einsum('hts,bcshd->bcthd', w_s, vc) + b_s.T[:, :, None]
    return u * mixed.reshape(b, L, D_CM)


def trunk_layer(x, cond, h0, w_ada, b_ada, norm1_g, w_in, conv_w, conv_b, dt_bias, A_log, d_skip,
                ssd_norm_g, cm_ln_g, cm_ln_b, cm_w_s, cm_b_s, w_out, norm2_g, w_ff1, w_ff2):
    b, L, _ = x.shape
    mod = jax.nn.silu(cond) @ w_ada + b_ada
    sh1, sc1, g1, sh2, sc2, g2 = [m[:, None, :] for m in jnp.split(mod, N_MOD, axis=-1)]
    h = rmsnorm(x, norm1_g) * (1 + sc1) + sh1
    proj = h @ w_in
    z, xbc, dt_raw, u, v = jnp.split(proj, SPLITS, axis=-1)
    y_ssd, h_fin = ssd_mixer(z, xbc, dt_raw.reshape(b, L, 2, N_SSD_HEADS), conv_w, conv_b,
                             dt_bias, A_log, d_skip, ssd_norm_g, h0)
    y_cm = chunk_mlp(u, v, cm_ln_g, cm_ln_b, cm_w_s, cm_b_s)
    x = x + g1 * (jnp.concatenate([y_ssd, y_cm], axis=-1) @ w_out)
    h = rmsnorm(x, norm2_g) * (1 + sc2) + sh2
    x = x + g2 * (jnp.square(jax.nn.relu(h @ w_ff1)) @ w_ff2)
    return x, h_fin


def setup_inputs(seed: int = 0) -> dict:
    key = jax.random.key(seed)
    ks = jax.random.split(key, 32)
    nrm = lambda k, shape, s: jax.random.normal(k, shape, jnp.float32) * s
    dt0 = jnp.exp(jax.random.uniform(ks[10], (DEPTH, 2, N_SSD_HEADS), jnp.float32,
                                     np.log(1e-3).astype(np.float32), np.log(1e-1).astype(np.float32)))
    return {
        "x_prompt": nrm(ks[0], (BATCH, SEQ, D_MODEL), 1.0),
        "x_sample": nrm(ks[1], (DEC_BATCH, DEC_SEQ, D_MODEL), 1.0),
        "state_ssd": nrm(ks[2], (DEC_BATCH, DEPTH, 2, N_SSD_HEADS, SSD_HEAD_DIM, D_STATE), 0.1),
        "c": nrm(ks[3], (DEC_BATCH, D_MODEL), 1.0),
        "c_ctx": nrm(ks[4], (D_MODEL,), 1.0),
        "w_ada": nrm(ks[5], (DEPTH, D_MODEL, N_MOD * D_MODEL), D_MODEL ** -0.5),
        "b_ada": nrm(ks[6], (DEPTH, N_MOD * D_MODEL), 0.02),
        "norm1_g": 1.0 + nrm(ks[7], (DEPTH, D_MODEL), 0.02),
        "w_in": nrm(ks[8], (DEPTH, D_MODEL, D_IN_PROJ), D_MODEL ** -0.5),
        "conv_w": nrm(ks[9], (DEPTH, D_CONV, D_XBC), D_CONV ** -0.5),
        "conv_b": nrm(ks[11], (DEPTH, D_XBC), 0.02),
        "dt_bias": dt0 + jnp.log(-jnp.expm1(-dt0)),
        "A_log": jnp.log(jax.random.uniform(ks[12], (DEPTH, 2, N_SSD_HEADS), jnp.float32, 1.0, 16.0)),
        "d_skip": 1.0 + nrm(ks[13], (DEPTH, N_SSD_HEADS), 0.02),
        "ssd_norm_g": 1.0 + nrm(ks[14], (DEPTH, D_SSD), 0.02),
        "cm_ln_g": 1.0 + nrm(ks[15], (DEPTH, D_CM), 0.02),
        "cm_ln_b": nrm(ks[16], (DEPTH, D_CM), 0.02),
        "cm_w_s": nrm(ks[17], (DEPTH, N_CM_HEADS, CM_CHUNK, CM_CHUNK), CM_CHUNK ** -0.5),
        "cm_b_s": 1.0 + nrm(ks[18], (DEPTH, N_CM_HEADS, CM_CHUNK), 0.02),
        "w_out": nrm(ks[19], (DEPTH, D_INNER, D_MODEL), D_INNER ** -0.5),
        "norm2_g": 1.0 + nrm(ks[20], (DEPTH, D_MODEL), 0.02),
        "w_ff1": nrm(ks[21], (DEPTH, D_MODEL, D_FF), D_MODEL ** -0.5),
        "w_ff2": nrm(ks[22], (DEPTH, D_FF, D_MODEL), D_FF ** -0.5),
        "final_norm_g": 1.0 + nrm(ks[23], (D_MODEL,), 0.02),
    }


def reference(x_prompt, x_sample, state_ssd, c, c_ctx, w_ada, b_ada, norm1_g, w_in, conv_w, conv_b,
              dt_bias, A_log, d_skip, ssd_norm_g, cm_ln_g, cm_ln_b, cm_w_s, cm_b_s, w_out, norm2_g,
              w_ff1, w_ff2, final_norm_g):
    layer_params = (w_ada, b_ada, norm1_g, w_in, conv_w, conv_b, dt_bias, A_log, d_skip, ssd_norm_g,
                    cm_ln_g, cm_ln_b, cm_w_s, cm_b_s, w_out, norm2_g, w_ff1, w_ff2)
    bp = x_prompt.shape[0]
    xp = x_prompt
    cond_ctx = jnp.broadcast_to(c_ctx, (bp, D_MODEL))
    h0_ctx = jnp.zeros((bp, 2, N_SSD_HEADS, SSD_HEAD_DIM, D_STATE), x_prompt.dtype)
    ctx_states = []
    for l in range(DEPTH):
        xp, st = trunk_layer(xp, cond_ctx, h0_ctx, *[w[l] for w in layer_params])
        ctx_states.append(st)
    y_prompt = rmsnorm(xp, final_norm_g)
    new_state_ssd = jnp.stack(ctx_states, axis=1)
    xs = x_sample + sincos_2d(x_sample.shape[1], x_sample.dtype)[None]
    for l in range(DEPTH):
        xs, _ = trunk_layer(xs, c, state_ssd[:, l], *[w[l] for w in layer_params])
    y_sample = rmsnorm(xs, final_norm_g)
    return (y_prompt, y_sample, new_state_ssd)
```

```python
import functools
import math

import jax
import jax.numpy as jnp
from jax import lax
from jax.experimental import pallas as pl
from jax.experimental.pallas import tpu as pltpu

F32 = jnp.float32
BF16 = jnp.bfloat16

D_MODEL = 1024
GRID_W = 64
D_SSD = 1024
D_CM = 1024
HEAD_DIM = 64
N_HEADS = 16
N_GROUPS = 4
D_STATE = 128
D_CONV = 5
CHUNK = 128
N_CM_HEADS = 8
D_FF = 4096
N_MOD = 6
EPS = 1e-6
D_XBC = D_SSD + 2 * N_GROUPS * D_STATE
LANES = 128
BF16_ROWS = 16
VMEM_LIMIT = 60 * 1024 * 1024
NEG_BIG = -1e30

TM = 512


def _sigmoid(x):
    return 1.0 / (1.0 + jnp.exp(-x))


def _silu(x):
    return x * _sigmoid(x)


def _gelu_tanh(x):
    return 0.5 * x * (1.0 + jnp.tanh(math.sqrt(2.0 / math.pi) * (x + 0.044715 * (x * x * x))))


def _softplus(x):
    return jnp.maximum(x, 0.0) + jnp.log1p(jnp.exp(-jnp.abs(x)))


def _rms(x, g):
    return x * lax.rsqrt(jnp.mean(x * x, axis=-1, keepdims=True) + EPS) * g


def _split2(x):
    hi = x.astype(BF16)
    lo = (x - hi.astype(F32)).astype(BF16)
    return hi, lo


def _split3(x):
    hi = x.astype(BF16)
    r = x - hi.astype(F32)
    mid = r.astype(BF16)
    lo = (r - mid.astype(F32)).astype(BF16)
    return hi, mid, lo


def _dot(a, b):
    return jnp.dot(a, b, preferred_element_type=F32)


def _resident(shape):
    nd = len(shape)
    return pl.BlockSpec(shape, lambda *_: (0,) * nd, pipeline_mode=pl.Buffered(1))


def _ada_kernel(cond_ref, w_ref, b_ref, o_ref):
    a_hi, a_lo = _split2(_silu(cond_ref[...]))
    w_hi, w_lo = _split2(w_ref[...])
    o_ref[...] = _dot(a_hi, w_hi) + _dot(a_lo, w_hi) + _dot(a_hi, w_lo) + b_ref[...]


def _ada(cond8, w_ada, b_ada):
    tn = 512
    n = w_ada.shape[1]
    return pl.pallas_call(
        _ada_kernel,
        grid=(n // tn,),
        in_specs=[
            pl.BlockSpec((8, D_MODEL), lambda j: (0, 0)),
            pl.BlockSpec((D_MODEL, tn), lambda j: (0, j)),
            pl.BlockSpec((1, tn), lambda j: (0, j)),
        ],
        out_specs=pl.BlockSpec((8, tn), lambda j: (0, j)),
        out_shape=jax.ShapeDtypeStruct((8, n), F32),
        name="ada",
    )(cond8, w_ada, b_ada)


def _pos_kernel(o_ref):
    nf = D_MODEL // 4
    k = lax.broadcasted_iota(jnp.int32, (GRID_W, nf), 0).astype(F32)
    j = lax.broadcasted_iota(jnp.int32, (GRID_W, nf), 1).astype(F32)
    omega = jnp.exp(j * (-math.log(10000.0) / nf))
    ang = k * omega
    o_ref[:, :nf] = jnp.sin(ang)
    o_ref[:, nf:] = jnp.cos(ang)


def _pos_table():
    return pl.pallas_call(
        _pos_kernel,
        out_shape=jax.ShapeDtypeStruct((GRID_W, D_MODEL // 2), F32),
        name="pos_table",
    )()


def _add_pos(x, e_ref, row0):
    half = D_MODEL // 2
    tm = x.shape[0]
    e = e_ref[...]
    parts = []
    for q in range(tm // GRID_W):
        er = e_ref[pl.ds(row0 + q, 1), :]
        pos = jnp.concatenate([jnp.broadcast_to(er, (GRID_W, half)), e], axis=1)
        parts.append(x[q * GRID_W:(q + 1) * GRID_W] + pos)
    return jnp.concatenate(parts, axis=0)


def _inproj_kernel(*refs, add_pos, seq_len):
    if add_pos:
        (x_ref, mod_ref, g_ref, wz_ref, wxbc_ref, wdt_ref, wu_ref, wv_ref, e_ref,
         z_ref, xbc_ref, dt_ref, u_ref, v_ref) = refs
    else:
        (x_ref, mod_ref, g_ref, wz_ref, wxbc_ref, wdt_ref, wu_ref, wv_ref,
         z_ref, xbc_ref, dt_ref, u_ref, v_ref) = refs
    x = x_ref[...]
    if add_pos:
        t_in = (pl.program_id(0) * TM) % seq_len
        x = _add_pos(x, e_ref, t_in // GRID_W)
    sh1 = mod_ref[0, 0:1, :]
    sc1 = mod_ref[0, 1:2, :]
    h = _rms(x, g_ref[...]) * (1.0 + sc1) + sh1
    h_hi, h_lo = _split2(h)
    z_ref[...] = _dot(h_hi, wz_ref[...]).astype(BF16)
    xbc_ref[...] = _dot(h_hi, wxbc_ref[...]).astype(BF16)
    u_ref[...] = _dot(h_hi, wu_ref[...]).astype(BF16)
    v_ref[...] = _dot(h_hi, wv_ref[...]).astype(BF16)
    d2 = _dot(h_hi, wdt_ref[...])
    d1 = _dot(h_lo, wdt_ref[:, :LANES])
    dt_ref[...] = d2[:, :LANES] + d2[:, LANES:] + d1


def _inproj(x2d, mod, g, wz, wxbc, wdt, wu, wv, e_tab, seq_len):
    t = x2d.shape[0]
    add_pos = e_tab is not None
    n_mod = mod.shape[0]
    mod_map = (lambda i: ((i * TM) // seq_len, 0, 0)) if n_mod > 1 else (lambda i: (0, 0, 0))
    row = lambda n: pl.BlockSpec((TM, n), lambda i: (i, 0))
    in_specs = [
        row(D_MODEL),
        pl.BlockSpec((1, N_MOD, D_MODEL), mod_map),
        _resident((1, D_MODEL)),
        _resident(wz.shape), _resident(wxbc.shape), _resident(wdt.shape),
        _resident(wu.shape), _resident(wv.shape),
    ]
    args = [x2d, mod, g, wz, wxbc, wdt, wu, wv]
    if add_pos:
        in_specs.append(_resident(e_tab.shape))
        args.append(e_tab)
    return pl.pallas_call(
        functools.partial(_inproj_kernel, add_pos=add_pos, seq_len=seq_len),
        grid=(t // TM,),
        in_specs=in_specs,
        out_specs=[row(D_SSD), row(D_XBC), row(LANES), row(D_CM), row(D_CM)],
        out_shape=[
            jax.ShapeDtypeStruct((t, D_SSD), BF16),
            jax.ShapeDtypeStruct((t, D_XBC), BF16),
            jax.ShapeDtypeStruct((t, LANES), F32),
            jax.ShapeDtypeStruct((t, D_CM), BF16),
            jax.ShapeDtypeStruct((t, D_CM), BF16),
        ],
        compiler_params=pltpu.CompilerParams(
            dimension_semantics=("arbitrary",), vmem_limit_bytes=VMEM_LIMIT),
        name="inproj_pos" if add_pos else "inproj",
    )(*args)


def _conv_chunk(prev_ref, cur_ref, next_ref, cw_ref, cb_ref, xc_ref, r0, c, nc):
    cw = 256
    pad = D_CONV // 2
    for c0 in range(0, D_XBC, cw):
        cols = slice(c0, c0 + cw)
        prev = jnp.where(c > 0, prev_ref[0, :, cols].astype(F32), 0.0)
        nxt = jnp.where(c < nc - 1, next_ref[0, :, cols].astype(F32), 0.0)
        xp = jnp.concatenate([prev, cur_ref[0, :, cols].astype(F32), nxt], axis=0)
        acc = jnp.broadcast_to(cb_ref[:, cols], (CHUNK, cw))
        for k in range(D_CONV):
            lo = BF16_ROWS - pad + k
            acc = acc + xp[lo:lo + CHUNK, :] * cw_ref[k:k + 1, cols]
        xc_ref[pl.ds(r0, CHUNK), cols] = _silu(acc).astype(BF16)


def _ssd_chunk(fwd, xcv, dtraw, dtb, alog, st_ref):
    q = CHUNK
    col0 = 0 if fwd else N_HEADS
    row_i = lax.broadcasted_iota(jnp.int32, (q, q), 0)
    col_i = lax.broadcasted_iota(jnp.int32, (q, q), 1)
    mask = (row_i >= col_i) if fwd else (row_i <= col_i)
    tri = jnp.where(mask, 1.0, 0.0).astype(BF16)

    dtv = _softplus(dtraw + dtb)
    dta = dtv * (-jnp.exp(alog))
    cs = _dot(tri, jnp.concatenate(_split3(dta), axis=1))
    acum = cs[:, :LANES] + cs[:, LANES:2 * LANES] + cs[:, 2 * LANES:]
    alast = acum[q - 1:q, :] if fwd else acum[0:1, :]
    wend = dtv * jnp.exp(alast - acum)
    cdec = jnp.exp(alast)
    arow = acum.T
    dtrow = dtv.T

    r_i = lax.broadcasted_iota(jnp.int32, (LANES, D_SSD), 0)
    l_i = lax.broadcasted_iota(jnp.int32, (LANES, D_SSD), 1)
    rep = jnp.where(r_i - col0 == lax.shift_right_logical(l_i, 6), 1.0, 0.0).astype(BF16)
    cd_hi, cd_lo = _split2(cdec)
    ex = _dot(jnp.concatenate([wend.astype(BF16),
                               jnp.broadcast_to(cd_hi, (BF16_ROWS, LANES)),
                               jnp.broadcast_to(cd_lo, (BF16_ROWS, LANES))], axis=0), rep)
    wexp = ex[:q]
    cdrow = ex[q:q + 1] + ex[q + BF16_ROWS:q + BF16_ROWS + 1]

    xs = xcv[:, :D_SSD]
    bm = xcv[:, D_SSD:D_SSD + N_GROUPS * D_STATE]
    cm = xcv[:, D_SSD + N_GROUPS * D_STATE:]
    lane2 = lax.broadcasted_iota(jnp.int32, (2 * q, LANES), 1)
    zero = jnp.zeros((2 * q, LANES), BF16)

    ys = []
    for g in range(N_GROUPS):
        bg = bm[:, g * D_STATE:(g + 1) * D_STATE]
        cg = cm[:, g * D_STATE:(g + 1) * D_STATE]
        cb = lax.dot_general(cg, bg, (((1,), (1,)), ((), ())), preferred_element_type=F32)
        cg32 = cg.astype(F32)
        for pp in range(2):
            p = 2 * g + pp
            rhs = jnp.concatenate([xs[:, p * LANES:(p + 1) * LANES],
                                   st_ref[:, p * LANES:(p + 1) * LANES].astype(BF16)], axis=0)
            acc = None
            for e in range(2):
                ch = col0 + 2 * p + e
                acol = jnp.broadcast_to(acum[:, ch:ch + 1], (q, q))
                seg = acol - arow[ch:ch + 1, :]
                m = cb * jnp.exp(jnp.where(mask, seg, NEG_BIG)) * dtrow[ch:ch + 1, :]
                lhs = jnp.concatenate([m, cg32 * jnp.exp(acol)], axis=1).astype(BF16)
                keep = (lane2 < HEAD_DIM) if e == 0 else (lane2 >= HEAD_DIM)
                t = _dot(lhs, jnp.where(keep, rhs, zero))
                acc = t if acc is None else acc + t
            ys.append(acc)
    y = jnp.concatenate(ys, axis=1)

    gw = D_SSD // N_GROUPS
    for g in range(N_GROUPS):
        cols = slice(g * gw, (g + 1) * gw)
        xw = (xs[:, cols].astype(F32) * wexp[:, cols]).astype(BF16)
        bg = bm[:, g * D_STATE:(g + 1) * D_STATE]
        ds = lax.dot_general(bg, xw, (((0,), (0,)), ((), ())), preferred_element_type=F32)
        st_ref[:, cols] = st_ref[:, cols] * cdrow[:, cols] + ds
    return y


def _mixer_kernel(prev_ref, cur_ref, next_ref, dt_ref, z_ref, u_ref, v_ref, h0_ref,
                  cw_ref, cb_ref, dtb_ref, alog_ref, dsk_ref, ng_ref, lng_ref, lnb_ref,
                  ws_ref, bs_ref, ycat_ref, hfin_ref, xc_ref, yb_ref, st_ref, *, nc):
    s = pl.program_id(1)
    i = pl.program_id(2)
    dtb = dtb_ref[...]
    alog = alog_ref[...]

    @pl.when(s == 0)
    def _backward_sweep():
        c = nc - 1 - i
        r0 = pl.multiple_of(c * CHUNK, CHUNK)

        @pl.when(i == 0)
        def _():
            st_ref[...] = h0_ref[0, 1].T

        _conv_chunk(prev_ref, cur_ref, next_ref, cw_ref, cb_ref, xc_ref, r0, c, nc)
        y = _ssd_chunk(False, xc_ref[pl.ds(r0, CHUNK), :], dt_ref[0], dtb, alog, st_ref)
        yb_ref[pl.ds(r0, CHUNK), :] = y.astype(BF16)

        @pl.when(i == nc - 1)
        def _():
            hfin_ref[0, 1] = st_ref[...].T

    @pl.when(s == 1)
    def _forward_sweep():
        r0 = pl.multiple_of(i * CHUNK, CHUNK)

        @pl.when(i == 0)
        def _():
            st_ref[...] = h0_ref[0, 0].T

        xcv = xc_ref[pl.ds(r0, CHUNK), :]
        y = _ssd_chunk(True, xcv, dt_ref[0], dtb, alog, st_ref)
        y = y + yb_ref[pl.ds(r0, CHUNK), :].astype(F32) + xcv[:, :D_SSD].astype(F32) * dsk_ref[...]
        y_ssd = _rms(y * _silu(z_ref[0].astype(F32)), ng_ref[...])

        v = _gelu_tanh(v_ref[0].astype(F32))
        mu = jnp.mean(v, axis=-1, keepdims=True)
        vc = v - mu
        var = jnp.mean(vc * vc, axis=-1, keepdims=True)
        vln = (vc * lax.rsqrt(var + EPS) * lng_ref[...] + lnb_ref[...]).astype(BF16)
        hd = D_CM // N_CM_HEADS
        mixed = jnp.concatenate(
            [_dot(ws_ref[h], vln[:, h * hd:(h + 1) * hd]) for h in range(N_CM_HEADS)], axis=1)
        y_cm = _gelu_tanh(u_ref[0].astype(F32)) * (mixed + bs_ref[...])

        ycat_ref[0, :, :D_SSD] = y_ssd.astype(BF16)
        ycat_ref[0, :, D_SSD:] = y_cm.astype(BF16)

        @pl.when(i == nc - 1)
        def _():
            hfin_ref[0, 0] = st_ref[...].T


def _mixer(xbc, dt, z, u, v, h0, conv_w, conv_b, dtb, alog, dsk, ng, lng, lnb, ws, bs):
    b, seq, _ = xbc.shape
    nc = seq // CHUNK
    tiles = CHUNK // BF16_ROWS
    n_tiles = seq // BF16_ROWS

    def chunk_of(s, i):
        return s * i + (1 - s) * (nc - 1 - i)

    fwd_blk = lambda w: pl.BlockSpec((1, CHUNK, w), lambda bi, s, i: (bi, s * i, 0))
    in_specs = [
        pl.BlockSpec((1, BF16_ROWS, D_XBC),
                     lambda bi, s, i: (bi, (1 - s) * jnp.maximum((nc - 1 - i) * tiles - 1, 0), 0)),
        pl.BlockSpec((1, CHUNK, D_XBC), lambda bi, s, i: (bi, (1 - s) * (nc - 1 - i), 0)),
        pl.BlockSpec((1, BF16_ROWS, D_XBC),
                     lambda bi, s, i: (bi, (1 - s) * jnp.minimum((nc - i) * tiles, n_tiles - 1), 0)),
        pl.BlockSpec((1, CHUNK, LANES), lambda bi, s, i: (bi, chunk_of(s, i), 0)),
        fwd_blk(D_SSD), fwd_blk(D_CM), fwd_blk(D_CM),
        pl.BlockSpec((1, 2, D_SSD, D_STATE), lambda bi, s, i: (bi, 0, 0, 0)),
        _resident(conv_w.shape), _resident(conv_b.shape), _resident(dtb.shape), _resident(alog.shape),
        _resident(dsk.shape), _resident(ng.shape), _resident(lng.shape), _resident(lnb.shape),
        _resident(ws.shape), _resident(bs.shape),
    ]
    return pl.pallas_call(
        functools.partial(_mixer_kernel, nc=nc),
        grid=(b, 2, nc),
        in_specs=in_specs,
        out_specs=[
            pl.BlockSpec((1, CHUNK, D_SSD + D_CM), lambda bi, s, i: (bi, s * i, 0)),
            pl.BlockSpec((1, 2, D_SSD, D_STATE), lambda bi, s, i: (bi, 0, 0, 0)),
        ],
        out_shape=[
            jax.ShapeDtypeStruct((b, seq, D_SSD + D_CM), BF16),
            jax.ShapeDtypeStruct((b, 2, D_SSD, D_STATE), F32),
        ],
        scratch_shapes=[
            pltpu.VMEM((seq, D_XBC), BF16),
            pltpu.VMEM((seq, D_SSD), BF16),
            pltpu.VMEM((D_STATE, D_SSD), F32),
        ],
        compiler_params=pltpu.CompilerParams(
            dimension_semantics=("arbitrary", "arbitrary", "arbitrary"),
            vmem_limit_bytes=VMEM_LIMIT),
        name=f"mixer_l{seq}",
    )(xbc, xbc, xbc, dt, z, u, v, h0, conv_w, conv_b, dtb, alog, dsk, ng, lng, lnb, ws, bs)


def _outproj_kernel(*refs, add_pos, seq_len):
    if add_pos:
        (y_ref, x_ref, mod_ref, wo_ref, g2_ref, w1_ref, w2_ref, gf_ref, e_ref, o_ref) = refs
    else:
        (y_ref, x_ref, mod_ref, wo_ref, g2_ref, w1_ref, w2_ref, gf_ref, o_ref) = refs
    x = x_ref[...]
    if add_pos:
        t_in = (pl.program_id(0) * TM) % seq_len
        x = _add_pos(x, e_ref, t_in // GRID_W)
    g1 = mod_ref[0, 2:3, :]
    sh2 = mod_ref[0, 3:4, :]
    sc2 = mod_ref[0, 4:5, :]
    g2 = mod_ref[0, 5:6, :]
    x = x + g1 * _dot(y_ref[...], wo_ref[...])
    h = (_rms(x, g2_ref[...]) * (1.0 + sc2) + sh2).astype(BF16)
    hid = jnp.maximum(_dot(h, w1_ref[...]), 0.0)
    hid = (hid * hid).astype(BF16)
    x = x + g2 * _dot(hid, w2_ref[...])
    o_ref[...] = _rms(x, gf_ref[...])


def _outproj(ycat, x2d, mod, wo, g2, w1, w2, gf, e_tab, seq_len):
    t = x2d.shape[0]
    add_pos = e_tab is not None
    n_mod = mod.shape[0]
    mod_map = (lambda i: ((i * TM) // seq_len, 0, 0)) if n_mod > 1 else (lambda i: (0, 0, 0))
    row = lambda n: pl.BlockSpec((TM, n), lambda i: (i, 0))
    in_specs = [
        row(D_SSD + D_CM), row(D_MODEL),
        pl.BlockSpec((1, N_MOD, D_MODEL), mod_map),
        _resident(wo.shape), _resident((1, D_MODEL)), _resident(w1.shape), _resident(w2.shape),
        _resident((1, D_MODEL)),
    ]
    args = [ycat, x2d, mod, wo, g2, w1, w2, gf]
    if add_pos:
        in_specs.append(_resident(e_tab.shape))
        args.append(e_tab)
    return pl.pallas_call(
        functools.partial(_outproj_kernel, add_pos=add_pos, seq_len=seq_len),
        grid=(t // TM,),
        in_specs=in_specs,
        out_specs=row(D_MODEL),
        out_shape=jax.ShapeDtypeStruct((t, D_MODEL), F32),
        compiler_params=pltpu.CompilerParams(
            dimension_semantics=("arbitrary",), vmem_limit_bytes=VMEM_LIMIT),
        name="outproj_pos" if add_pos else "outproj",
    )(*args)


def _layer(x, mod, h0, e_tab, p):
    b, seq, _ = x.shape
    x2d = x.reshape(b * seq, D_MODEL)
    z, xbc, dt, u, v = _inproj(x2d, mod, p["norm1_g"], p["wz"], p["wxbc"], p["wdt"], p["wu"], p["wv"],
                               e_tab, seq)
    r3 = lambda a: a.reshape(b, seq, a.shape[-1])
    ycat, hfin = _mixer(r3(xbc), r3(dt), r3(z), r3(u), r3(v), h0,
                        p["conv_w"], p["conv_b"], p["dtb"], p["alog"], p["dsk"], p["ssd_norm_g"],
                        p["cm_ln_g"], p["cm_ln_b"], p["ws"], p["bs"])
    out = _outproj(ycat.reshape(b * seq, D_SSD + D_CM), x2d, mod, p["wo"], p["norm2_g"],
                   p["w1"], p["w2"], p["final_norm_g"], e_tab, seq)
    return out.reshape(b, seq, D_MODEL), hfin


def kernel(x_prompt, x_sample, state_ssd, c, c_ctx, w_ada, b_ada, norm1_g, w_in, conv_w, conv_b,
           dt_bias, A_log, d_skip, ssd_norm_g, cm_ln_g, cm_ln_b, cm_w_s, cm_b_s, w_out, norm2_g,
           w_ff1, w_ff2, final_norm_g):
    bp = x_prompt.shape[0]
    bs_ = x_sample.shape[0]
    lyr = 0

    cond8 = jnp.concatenate([c_ctx[None, :], c, jnp.zeros((8 - 1 - bs_, D_MODEL), F32)], axis=0)
    mod8 = _ada(cond8, w_ada[lyr], b_ada[lyr][None, :])
    mod_ctx = mod8[0:1].reshape(1, N_MOD, D_MODEL)
    mod_smp = mod8[1:1 + bs_].reshape(bs_, N_MOD, D_MODEL)

    w = w_in[lyr]
    o_dt = D_SSD + D_XBC
    o_u = o_dt + 2 * N_HEADS
    wdt = jnp.pad(w[:, o_dt:o_u], ((0, 0), (0, LANES - 2 * N_HEADS)))
    wdt_hi = wdt.astype(BF16)
    wdt_lo = (wdt - wdt_hi.astype(F32)).astype(BF16)
    row = lambda a: a.reshape(1, -1)
    lane_pad = lambda a: jnp.pad(a.reshape(1, -1), ((0, 0), (0, LANES - 2 * N_HEADS)))
    p = dict(
        norm1_g=row(norm1_g[lyr]),
        wz=w[:, :D_SSD].astype(BF16),
        wxbc=w[:, D_SSD:o_dt].astype(BF16),
        wdt=jnp.concatenate([wdt_hi, wdt_lo], axis=1),
        wu=w[:, o_u:o_u + D_CM].astype(BF16),
        wv=w[:, o_u + D_CM:].astype(BF16),
        conv_w=conv_w[lyr], conv_b=row(conv_b[lyr]),
        dtb=lane_pad(dt_bias[lyr]), alog=lane_pad(A_log[lyr]),
        dsk=row(jnp.repeat(d_skip[lyr], HEAD_DIM)),
        ssd_norm_g=row(ssd_norm_g[lyr]), cm_ln_g=row(cm_ln_g[lyr]), cm_ln_b=row(cm_ln_b[lyr]),
        ws=cm_w_s[lyr].astype(BF16),
        bs=jnp.repeat(cm_b_s[lyr].T, D_CM // N_CM_HEADS, axis=1),
        wo=w_out[lyr].astype(BF16), norm2_g=row(norm2_g[lyr]),
        w1=w_ff1[lyr].astype(BF16), w2=w_ff2[lyr].astype(BF16),
        final_norm_g=row(final_norm_g),
    )

    h0_ctx = jnp.zeros((bp, 2, D_SSD, D_STATE), F32)
    y_prompt, st = _layer(x_prompt, mod_ctx, h0_ctx, None, p)
    new_state = st.reshape(bp, 1, 2, N_HEADS, HEAD_DIM, D_STATE)

    h0_smp = state_ssd[:, lyr].reshape(bs_, 2, D_SSD, D_STATE)
    y_sample, _ = _layer(x_sample, mod_smp, h0_smp, _pos_table(), p)
    return (y_prompt, y_sample, new_state)
```

```python
import functools
import math

import jax
import jax.numpy as jnp
from jax import lax
from jax.experimental import pallas as pl
from jax.experimental.pallas import tpu as pltpu

F32 = jnp.float32
BF16 = jnp.bfloat16

D_MODEL = 1024
GRID_W = 64
D_SSD = 1024
D_CM = 1024
HEAD_DIM = 64
N_HEADS = 16
N_GROUPS = 4
D_STATE = 128
D_CONV = 5
CHUNK = 128
N_CM_HEADS = 8
D_FF = 4096
N_MOD = 6
EPS = 1e-6
D_XBC = D_SSD + 2 * N_GROUPS * D_STATE
LANES = 128
HALO = 16
VMEM_LIMIT = 60 * 1024 * 1024
NEG_BIG = -1e30
LOG2E = 1.0 / math.log(2.0)

TM = 512


def _sigmoid(x):
    return 1.0 / (1.0 + jnp.exp(-x))


def _silu(x):
    return x * _sigmoid(x)


def _gelu_tanh(x):
    return 0.5 * x * (1.0 + jnp.tanh(math.sqrt(2.0 / math.pi) * (x + 0.044715 * (x * x * x))))


def _softplus(x):
    return jnp.maximum(x, 0.0) + jnp.log1p(jnp.exp(-jnp.abs(x)))


def _rms(x, g):
    return x * lax.rsqrt(jnp.mean(x * x, axis=-1, keepdims=True) + EPS) * g


def _split2(x):
    hi = x.astype(BF16)
    lo = (x - hi.astype(F32)).astype(BF16)
    return hi, lo


def _split3(x):
    hi = x.astype(BF16)
    r = x - hi.astype(F32)
    mid = r.astype(BF16)
    lo = (r - mid.astype(F32)).astype(BF16)
    return hi, mid, lo


def _dot(a, b):
    return jnp.dot(a, b, preferred_element_type=F32)


def _resident(shape):
    nd = len(shape)
    return pl.BlockSpec(shape, lambda *_: (0,) * nd, pipeline_mode=pl.Buffered(1))


def _ada_kernel(cond_ref, w_ref, b_ref, o_ref):
    a_hi, a_lo = _split2(_silu(cond_ref[...]))
    w_hi, w_lo = _split2(w_ref[...])
    o_ref[...] = _dot(a_hi, w_hi) + _dot(a_lo, w_hi) + _dot(a_hi, w_lo) + b_ref[...]


def _ada(cond8, w_ada, b_ada):
    tn = 512
    n = w_ada.shape[1]
    return pl.pallas_call(
        _ada_kernel,
        grid=(n // tn,),
        in_specs=[
            pl.BlockSpec((8, D_MODEL), lambda j: (0, 0)),
            pl.BlockSpec((D_MODEL, tn), lambda j: (0, j)),
            pl.BlockSpec((1, tn), lambda j: (0, j)),
        ],
        out_specs=pl.BlockSpec((8, tn), lambda j: (0, j)),
        out_shape=jax.ShapeDtypeStruct((8, n), F32),
        name="ada",
    )(cond8, w_ada, b_ada)


def _pos_kernel(o_ref):
    nf = D_MODEL // 4
    k = lax.broadcasted_iota(jnp.int32, (GRID_W, nf), 0).astype(F32)
    j = lax.broadcasted_iota(jnp.int32, (GRID_W, nf), 1).astype(F32)
    omega = jnp.exp(j * (-math.log(10000.0) / nf))
    ang = k * omega
    o_ref[:, :nf] = jnp.sin(ang)
    o_ref[:, nf:] = jnp.cos(ang)


def _pos_table():
    return pl.pallas_call(
        _pos_kernel,
        out_shape=jax.ShapeDtypeStruct((GRID_W, D_MODEL // 2), F32),
        name="pos_table",
    )()


def _pos_rows(e_ref, grid_row, col0, n):
    half = D_MODEL // 2
    er = e_ref[pl.ds(jnp.clip(grid_row, 0, GRID_W - 1), 1), :]
    return jnp.concatenate([jnp.broadcast_to(er, (n, half)), e_ref[col0:col0 + n, :]], axis=1)


def _inproj_kernel(*refs, add_pos, halo, seq_len, tm):
    refs = list(refs)
    x_ref = refs.pop(0)
    xp_ref, xn_ref = (refs.pop(0), refs.pop(0)) if halo else (None, None)
    mod_ref, g_ref, wz_ref, wxbc_ref, wdt_ref, wu_ref, wv_ref, cw_ref, cb_ref = refs[:9]
    refs = refs[9:]
    e_ref = refs.pop(0) if add_pos else None
    z_ref, xc_ref, dt_ref, u_ref, v_ref, xs_ref = refs

    t_in = (pl.program_id(0) * tm) % seq_len
    off = HALO if halo else 0
    x = x_ref[...]
    if halo:
        x = jnp.concatenate([xp_ref[...], x, xn_ref[...]], axis=0)
    if add_pos:
        row0 = t_in // GRID_W
        pos = [_pos_rows(e_ref, row0 + q, 0, GRID_W) for q in range(tm // GRID_W)]
        if halo:
            pos = ([_pos_rows(e_ref, row0 - 1, GRID_W - HALO, HALO)] + pos
                   + [_pos_rows(e_ref, row0 + tm // GRID_W, 0, HALO)])
        x = x + jnp.concatenate(pos, axis=0)
    sh1 = mod_ref[0, 0:1, :]
    sc1 = mod_ref[0, 1:2, :]
    h = _rms(x, g_ref[...]) * (1.0 + sc1) + sh1
    h_hi, h_lo = _split2(h)

    zero_halo = jnp.zeros((HALO, D_XBC), F32)
    if halo:
        xs_ref[...] = _dot(h_hi, wxbc_ref[...])

        @pl.when(t_in == 0)
        def _():
            xs_ref[0:HALO, :] = zero_halo

        @pl.when(t_in + tm == seq_len)
        def _():
            xs_ref[HALO + tm:, :] = zero_halo
    else:
        xs_ref[0:HALO, :] = zero_halo
        xs_ref[HALO + tm:, :] = zero_halo
        xs_ref[HALO:HALO + tm, :] = _dot(h_hi, wxbc_ref[...])

    hm_hi = h_hi[off:off + tm]
    hm_lo = h_lo[off:off + tm]
    z_ref[...] = _dot(hm_hi, wz_ref[...]).astype(BF16)
    u_ref[...] = _dot(hm_hi, wu_ref[...]).astype(BF16)
    v_ref[...] = _dot(hm_hi, wv_ref[...]).astype(BF16)
    d2 = _dot(hm_hi, wdt_ref[...])
    d1 = _dot(hm_lo, wdt_ref[:, :LANES])
    dt_ref[...] = d2[:, :LANES] + d2[:, LANES:] + d1

    cw, rb = 256, 64
    pad = D_CONV // 2
    for c0 in range(0, D_XBC, cw):
        cols = slice(c0, c0 + cw)
        taps = [cw_ref[k:k + 1, cols] for k in range(D_CONV)]
        bias = cb_ref[:, cols]
        for r0 in range(0, tm, rb):
            acc = jnp.broadcast_to(bias, (rb, cw))
            for k in range(D_CONV):
                lo = HALO - pad + k + r0
                acc = acc + xs_ref[lo:lo + rb, cols] * taps[k]
            xc_ref[r0:r0 + rb, cols] = _silu(acc).astype(BF16)


def _inproj(x2d, mod, g, wz, wxbc, wdt, wu, wv, conv_w, conv_b, e_tab, seq_len):
    t = x2d.shape[0]
    tm = min(TM, seq_len)
    halo = seq_len > tm
    add_pos = e_tab is not None
    n_mod = mod.shape[0]
    mod_map = (lambda i: ((i * tm) // seq_len, 0, 0)) if n_mod > 1 else (lambda i: (0, 0, 0))
    row = lambda n: pl.BlockSpec((tm, n), lambda i: (i, 0))
    hpt = tm // HALO
    last = t // HALO - 1
    in_specs = [row(D_MODEL)]
    args = [x2d]
    if halo:
        in_specs += [
            pl.BlockSpec((HALO, D_MODEL), lambda i: (jnp.maximum(i * hpt - 1, 0), 0)),
            pl.BlockSpec((HALO, D_MODEL), lambda i: (jnp.minimum((i + 1) * hpt, last), 0)),
        ]
        args += [x2d, x2d]
    in_specs += [
        pl.BlockSpec((1, N_MOD, D_MODEL), mod_map),
        _resident((1, D_MODEL)),
        _resident(wz.shape), _resident(wxbc.shape), _resident(wdt.shape),
        _resident(wu.shape), _resident(wv.shape),
        _resident(conv_w.shape), _resident(conv_b.shape),
    ]
    args += [mod, g, wz, wxbc, wdt, wu, wv, conv_w, conv_b]
    if add_pos:
        in_specs.append(_resident(e_tab.shape))
        args.append(e_tab)
    return pl.pallas_call(
        functools.partial(_inproj_kernel, add_pos=add_pos, halo=halo, seq_len=seq_len, tm=tm),
        grid=(t // tm,),
        in_specs=in_specs,
        out_specs=[row(D_SSD), row(D_XBC), row(LANES), row(D_CM), row(D_CM)],
        out_shape=[
            jax.ShapeDtypeStruct((t, D_SSD), BF16),
            jax.ShapeDtypeStruct((t, D_XBC), BF16),
            jax.ShapeDtypeStruct((t, LANES), F32),
            jax.ShapeDtypeStruct((t, D_CM), BF16),
            jax.ShapeDtypeStruct((t, D_CM), BF16),
        ],
        scratch_shapes=[pltpu.VMEM((tm + 2 * HALO, D_XBC), F32)],
        compiler_params=pltpu.CompilerParams(
            dimension_semantics=("arbitrary",), vmem_limit_bytes=VMEM_LIMIT),
        name="inproj_pos" if add_pos else "inproj",
    )(*args)


def _ssd_chunk(fwd, xcv, dtraw, dtb, alog, st_ref):
    q = CHUNK
    col0 = 0 if fwd else N_HEADS
    row_i = lax.broadcasted_iota(jnp.int32, (q, q), 0)
    col_i = lax.broadcasted_iota(jnp.int32, (q, q), 1)
    mask = (row_i >= col_i) if fwd else (row_i <= col_i)
    tri = jnp.where(mask, 1.0, 0.0).astype(BF16)

    dtv = _softplus(dtraw + dtb)
    dta = dtv * (-jnp.exp(alog))
    cs = _dot(tri, jnp.concatenate(_split3(dta), axis=1))
    acum = cs[:, :LANES] + cs[:, LANES:2 * LANES] + cs[:, 2 * LANES:]
    alast = acum[q - 1:q, :] if fwd else acum[0:1, :]
    wend = dtv * jnp.exp(alast - acum)
    cdec = jnp.exp(alast)
    acum2 = acum * LOG2E
    arow2 = acum2.T
    dtrow = dtv.T.astype(BF16)

    r_i = lax.broadcasted_iota(jnp.int32, (LANES, D_SSD), 0)
    l_i = lax.broadcasted_iota(jnp.int32, (LANES, D_SSD), 1)
    rep = jnp.where(r_i - col0 == lax.shift_right_logical(l_i, 6), 1.0, 0.0).astype(BF16)
    cd_hi, cd_lo = _split2(cdec)
    ex = _dot(jnp.concatenate([wend.astype(BF16),
                               jnp.broadcast_to(cd_hi, (HALO, LANES)),
                               jnp.broadcast_to(cd_lo, (HALO, LANES))], axis=0), rep)
    wexp = ex[:q]
    cdrow = ex[q:q + 1] + ex[q + HALO:q + HALO + 1]

    xs = xcv[:, :D_SSD]
    bm = xcv[:, D_SSD:D_SSD + N_GROUPS * D_STATE]
    cm = xcv[:, D_SSD + N_GROUPS * D_STATE:]
    lane2 = lax.broadcasted_iota(jnp.int32, (2 * q, LANES), 1)
    zero = jnp.zeros((2 * q, LANES), BF16)

    ys = []
    for g in range(N_GROUPS):
        bg = bm[:, g * D_STATE:(g + 1) * D_STATE]
        cg = cm[:, g * D_STATE:(g + 1) * D_STATE]
        cb = lax.dot_general(cg, bg, (((1,), (1,)), ((), ())), preferred_element_type=F32).astype(BF16)
        for pp in range(2):
            p = 2 * g + pp
            rhs = jnp.concatenate([xs[:, p * LANES:(p + 1) * LANES],
                                   st_ref[:, p * LANES:(p + 1) * LANES].astype(BF16)], axis=0)
            acc = None
            for e in range(2):
                ch = col0 + 2 * p + e
                acol2 = jnp.broadcast_to(acum2[:, ch:ch + 1], (q, q))
                seg2 = acol2 - arow2[ch:ch + 1, :]
                decay = jnp.exp2(jnp.where(mask, seg2, NEG_BIG)).astype(BF16)
                m = cb * decay * dtrow[ch:ch + 1, :]
                lhs = jnp.concatenate([m, cg * jnp.exp2(acol2).astype(BF16)], axis=1)
                keep = (lane2 < HEAD_DIM) if e == 0 else (lane2 >= HEAD_DIM)
                t = _dot(lhs, jnp.where(keep, rhs, zero))
                acc = t if acc is None else acc + t
            ys.append(acc)
    y = jnp.concatenate(ys, axis=1)

    gw = D_SSD // N_GROUPS
    for g in range(N_GROUPS):
        cols = slice(g * gw, (g + 1) * gw)
        xw = (xs[:, cols].astype(F32) * wexp[:, cols]).astype(BF16)
        bg = bm[:, g * D_STATE:(g + 1) * D_STATE]
        ds = lax.dot_general(bg, xw, (((0,), (0,)), ((), ())), preferred_element_type=F32)
        st_ref[:, cols] = st_ref[:, cols] * cdrow[:, cols] + ds
    return y


def _mixer_kernel(*refs, nc, has_h0, emit_state):
    refs = list(refs)
    xc_ref, dt_ref = refs.pop(0), refs.pop(0)
    h0_ref = refs.pop(0) if has_h0 else None
    dtb_ref, alog_ref, dsk_ref, y_ref = refs[:4]
    refs = refs[4:]
    hfin_ref = refs.pop(0) if emit_state else None
    yb_ref, st_ref = refs

    s = pl.program_id(1)
    i = pl.program_id(2)
    dtb = dtb_ref[...]
    alog = alog_ref[...]

    def init_state(d):
        st_ref[...] = h0_ref[0, d].T if has_h0 else jnp.zeros((D_STATE, D_SSD), F32)

    @pl.when(s == 0)
    def _backward_sweep():
        r0 = pl.multiple_of((nc - 1 - i) * CHUNK, CHUNK)

        @pl.when(i == 0)
        def _():
            init_state(1)

        y = _ssd_chunk(False, xc_ref[0], dt_ref[0], dtb, alog, st_ref)
        yb_ref[pl.ds(r0, CHUNK), :] = y.astype(BF16)

        if emit_state:
            @pl.when(i == nc - 1)
            def _():
                hfin_ref[0, 1] = st_ref[...].T

    @pl.when(s == 1)
    def _forward_sweep():
        r0 = pl.multiple_of(i * CHUNK, CHUNK)

        @pl.when(i == 0)
        def _():
            init_state(0)

        xcv = xc_ref[0]
        y = _ssd_chunk(True, xcv, dt_ref[0], dtb, alog, st_ref)
        y = y + yb_ref[pl.ds(r0, CHUNK), :].astype(F32) + xcv[:, :D_SSD].astype(F32) * dsk_ref[...]
        y_ref[0] = y.astype(BF16)

        if emit_state:
            @pl.when(i == nc - 1)
            def _():
                hfin_ref[0, 0] = st_ref[...].T


def _mixer(xc, dt, h0, dtb, alog, dsk, emit_state):
    b, seq, _ = xc.shape
    nc = seq // CHUNK
    has_h0 = h0 is not None

    def chunk_of(s, i):
        return s * i + (1 - s) * (nc - 1 - i)

    state_spec = pl.BlockSpec((1, 2, D_SSD, D_STATE), lambda bi, s, i: (bi, 0, 0, 0))
    in_specs = [
        pl.BlockSpec((1, CHUNK, D_XBC), lambda bi, s, i: (bi, chunk_of(s, i), 0)),
        pl.BlockSpec((1, CHUNK, LANES), lambda bi, s, i: (bi, chunk_of(s, i), 0)),
    ]
    args = [xc, dt]
    if has_h0:
        in_specs.append(state_spec)
        args.append(h0)
    in_specs += [_resident(dtb.shape), _resident(alog.shape), _resident(dsk.shape)]
    args += [dtb, alog, dsk]
    out_specs = [pl.BlockSpec((1, CHUNK, D_SSD), lambda bi, s, i: (bi, s * i, 0))]
    out_shape = [jax.ShapeDtypeStruct((b, seq, D_SSD), BF16)]
    if emit_state:
        out_specs.append(state_spec)
        out_shape.append(jax.ShapeDtypeStruct((b, 2, D_SSD, D_STATE), F32))
    return pl.pallas_call(
        functools.partial(_mixer_kernel, nc=nc, has_h0=has_h0, emit_state=emit_state),
        grid=(b, 2, nc),
        in_specs=in_specs,
        out_specs=out_specs,
        out_shape=out_shape,
        scratch_shapes=[
            pltpu.VMEM((seq, D_SSD), BF16),
            pltpu.VMEM((D_STATE, D_SSD), F32),
        ],
        compiler_params=pltpu.CompilerParams(
            dimension_semantics=("arbitrary", "arbitrary", "arbitrary"),
            vmem_limit_bytes=VMEM_LIMIT),
        name=f"mixer_l{seq}",
    )(*args)


def _outproj_kernel(*refs, add_pos, seq_len):
    refs = list(refs)
    e_ref = refs.pop(-2) if add_pos else None
    (y_ref, z_ref, u_ref, v_ref, x_ref, mod_ref, ng_ref, lng_ref, lnb_ref, ws_ref, bs_ref,
     wo_ref, g2_ref, w1_ref, w2_ref, gf_ref, o_ref) = refs
    nck = TM // CHUNK
    hd = D_CM // N_CM_HEADS

    x = x_ref[...]
    if add_pos:
        row0 = ((pl.program_id(0) * TM) % seq_len) // GRID_W
        x = x + jnp.concatenate([_pos_rows(e_ref, row0 + q, 0, GRID_W) for q in range(TM // GRID_W)], axis=0)

    y_ssd = _rms(y_ref[...].astype(F32) * _silu(z_ref[...].astype(F32)), ng_ref[...]).astype(BF16)

    v = _gelu_tanh(v_ref[...].astype(F32))
    vc = v - jnp.mean(v, axis=-1, keepdims=True)
    var = jnp.mean(vc * vc, axis=-1, keepdims=True)
    vln = (vc * lax.rsqrt(var + EPS) * lng_ref[...] + lnb_ref[...]).astype(BF16)
    mixed = [[None] * N_CM_HEADS for _ in range(nck)]
    for h in range(N_CM_HEADS):
        vh = jnp.concatenate([vln[k * CHUNK:(k + 1) * CHUNK, h * hd:(h + 1) * hd] for k in range(nck)], axis=1)
        mh = _dot(ws_ref[h], vh)
        for k in range(nck):
            mixed[k][h] = mh[:, k * hd:(k + 1) * hd]
    bs = bs_ref[...]
    mixed = jnp.concatenate([jnp.concatenate(mk, axis=1) + bs for mk in mixed], axis=0)
    y_cm = (_gelu_tanh(u_ref[...].astype(F32)) * mixed).astype(BF16)

    g1 = mod_ref[0, 2:3, :]
    sh2 = mod_ref[0, 3:4, :]
    sc2 = mod_ref[0, 4:5, :]
    g2 = mod_ref[0, 5:6, :]
    x = x + g1 * _dot(jnp.concatenate([y_ssd, y_cm], axis=1), wo_ref[...])
    h = (_rms(x, g2_ref[...]) * (1.0 + sc2) + sh2).astype(BF16)
    hid = jnp.maximum(_dot(h, w1_ref[...]), 0.0)
    hid = (hid * hid).astype(BF16)
    x = x + g2 * _dot(hid, w2_ref[...])
    o_ref[...] = _rms(x, gf_ref[...])


def _outproj(y, z, u, v, x2d, mod, ng, lng, lnb, ws, bs, wo, g2, w1, w2, gf, e_tab, seq_len):
    t = x2d.shape[0]
    add_pos = e_tab is not None
    n_mod = mod.shape[0]
    mod_map = (lambda i: ((i * TM) // seq_len, 0, 0)) if n_mod > 1 else (lambda i: (0, 0, 0))
    row = lambda n: pl.BlockSpec((TM, n), lambda i: (i, 0))
    vec = _resident((1, D_MODEL))
    in_specs = [
        row(D_SSD), row(D_SSD), row(D_CM), row(D_CM), row(D_MODEL),
        pl.BlockSpec((1, N_MOD, D_MODEL), mod_map),
        vec, vec, vec, _resident(ws.shape), _resident(bs.shape),
        _resident(wo.shape), vec, _resident(w1.shape), _resident(w2.shape), vec,
    ]
    args = [y, z, u, v, x2d, mod, ng, lng, lnb, ws, bs, wo, g2, w1, w2, gf]
    if add_pos:
        in_specs.append(_resident(e_tab.shape))
        args.append(e_tab)
    return pl.pallas_call(
        functools.partial(_outproj_kernel, add_pos=add_pos, seq_len=seq_len),
        grid=(t // TM,),
        in_specs=in_specs,
        out_specs=row(D_MODEL),
        out_shape=jax.ShapeDtypeStruct((t, D_MODEL), F32),
        compiler_params=pltpu.CompilerParams(
            dimension_semantics=("arbitrary",), vmem_limit_bytes=VMEM_LIMIT),
        name="outproj_pos" if add_pos else "outproj",
    )(*args)


def _layer(x, mod, h0, e_tab, p, emit_state):
    b, seq, _ = x.shape
    x2d = x.reshape(b * seq, D_MODEL)
    z, xc, dt, u, v = _inproj(x2d, mod, p["norm1_g"], p["wz"], p["wxbc"], p["wdt"], p["wu"], p["wv"],
                              p["conv_w"], p["conv_b"], e_tab, seq)
    r3 = lambda a: a.reshape(b, seq, a.shape[-1])
    res = _mixer(r3(xc), r3(dt), h0, p["dtb"], p["alog"], p["dsk"], emit_state)
    y = res[0].reshape(b * seq, D_SSD)
    out = _outproj(y, z, u, v, x2d, mod, p["ssd_norm_g"], p["cm_ln_g"], p["cm_ln_b"], p["ws"], p["bs"],
                   p["wo"], p["norm2_g"], p["w1"], p["w2"], p["final_norm_g"], e_tab, seq)
    return out.reshape(b, seq, D_MODEL), (res[1] if emit_state else None)


def kernel(x_prompt, x_sample, state_ssd, c, c_ctx, w_ada, b_ada, norm1_g, w_in, conv_w, conv_b,
           dt_bias, A_log, d_skip, ssd_norm_g, cm_ln_g, cm_ln_b, cm_w_s, cm_b_s, w_out, norm2_g,
           w_ff1, w_ff2, final_norm_g):
    bp = x_prompt.shape[0]
    bs_ = x_sample.shape[0]
    lyr = 0

    cond8 = jnp.concatenate([c_ctx[None, :], c, jnp.zeros((8 - 1 - bs_, D_MODEL), F32)], axis=0)
    mod8 = _ada(cond8, w_ada[lyr], b_ada[lyr][None, :])
    mod_ctx = mod8[0:1].reshape(1, N_MOD, D_MODEL)
    mod_smp = mod8[1:1 + bs_].reshape(bs_, N_MOD, D_MODEL)

    w = w_in[lyr]
    o_dt = D_SSD + D_XBC
    o_u = o_dt + 2 * N_HEADS
    wdt = jnp.pad(w[:, o_dt:o_u], ((0, 0), (0, LANES - 2 * N_HEADS)))
    wdt_hi = wdt.astype(BF16)
    wdt_lo = (wdt - wdt_hi.astype(F32)).astype(BF16)
    row = lambda a: a.reshape(1, -1)
    lane_pad = lambda a: jnp.pad(a.reshape(1, -1), ((0, 0), (0, LANES - 2 * N_HEADS)))
    p = dict(
        norm1_g=row(norm1_g[lyr]),
        wz=w[:, :D_SSD].astype(BF16),
        wxbc=w[:, D_SSD:o_dt].astype(BF16),
        wdt=jnp.concatenate([wdt_hi, wdt_lo], axis=1),
        wu=w[:, o_u:o_u + D_CM].astype(BF16),
        wv=w[:, o_u + D_CM:].astype(BF16),
        conv_w=conv_w[lyr], conv_b=row(conv_b[lyr]),
        dtb=lane_pad(dt_bias[lyr]), alog=lane_pad(A_log[lyr]),
        dsk=row(jnp.repeat(d_skip[lyr], HEAD_DIM)),
        ssd_norm_g=row(ssd_norm_g[lyr]), cm_ln_g=row(cm_ln_g[lyr]), cm_ln_b=row(cm_ln_b[lyr]),
        ws=cm_w_s[lyr].astype(BF16),
        bs=jnp.repeat(cm_b_s[lyr].T, D_CM // N_CM_HEADS, axis=1),
        wo=w_out[lyr].astype(BF16), norm2_g=row(norm2_g[lyr]),
        w1=w_ff1[lyr].astype(BF16), w2=w_ff2[lyr].astype(BF16),
        final_norm_g=row(final_norm_g),
    )

    y_prompt, st = _layer(x_prompt, mod_ctx, None, None, p, True)
    new_state = st.reshape(bp, 1, 2, N_HEADS, HEAD_DIM, D_STATE)

    h0_smp = state_ssd[:, lyr].reshape(bs_, 2, D_SSD, D_STATE)
    y_sample, _ = _layer(x_sample, mod_smp, h0_smp, _pos_table(), p, False)
    return (y_prompt, y_sample, new_state)
```
